```python
import jax, jax.numpy as jnp
from jax import lax
import numpy as np

D_MODEL = 1024
BATCH = 8
SEQ = 2048
DEPTH = 2
DEC_BATCH = 32
DEC_SEQ = 4
PAST_LEN = 16384
PAGE_SIZE = 128

N_A_LAYERS = DEPTH // 2
N_B_LAYERS = DEPTH - N_A_LAYERS
N_DENSE = (DEPTH + 1) // 2
N_MOE = DEPTH // 2

C_CONV = 3 * D_MODEL // 4
CONV_WIDTH = 31
MEM_LEN = 256
MEM_HEADS = 4
MEM_HEAD_DIM = 64
MEM_WIDTH = MEM_HEADS * MEM_HEAD_DIM
HEAD_DIM = 64
NSA_HEADS = 12
NSA_KV_HEADS = 4
HEADS_PER_GROUP = NSA_HEADS // NSA_KV_HEADS
NSA_WIDTH = NSA_HEADS * HEAD_DIM
N_BRANCH = 3
CMP_BLOCK = 32
CMP_STRIDE = 16
CMP_HIDDEN = 64
SEL_BLOCK = 64
SEL_TOP_N = 16
CHUNKS_PER_SEL = SEL_BLOCK // CMP_STRIDE
WINDOW = 512
SEL_Q_BLOCK = 32
WIN_Q_BLOCK = 128
FORCE_SCORE = 1e4
A_IN = 2 * C_CONV + MEM_WIDTH
B_IN = NSA_WIDTH + N_BRANCH * NSA_HEADS + MEM_WIDTH
MIX_WIDTH = C_CONV + MEM_WIDTH
D_FF = 2816
N_EXPERTS = 8
TOP_K = 2
D_EXPERT = 1408
MOE_BLOCK = 128
EPS = 1e-6
ATTN_SCALE = HEAD_DIM ** -0.5
MEM_SCALE = MEM_HEAD_DIM ** -0.5

kernel_name = 'yoco_conformer_nsa_decoder_step'


def rmsnorm(x, g):
    xf = x.astype(jnp.float32)
    y = xf * lax.rsqrt(jnp.mean(xf * xf, axis=-1, keepdims=True) + EPS)
    return (y * g.astype(jnp.float32)).astype(x.dtype)


def layernorm(x, g, b):
    xf = x.astype(jnp.float32)
    mu = jnp.mean(xf, axis=-1, keepdims=True)
    var = jnp.mean(jnp.square(xf - mu), axis=-1, keepdims=True)
    y = (xf - mu) * lax.rsqrt(var + EPS) * g.astype(jnp.float32) + b.astype(jnp.float32)
    return y.astype(x.dtype)


def masked_softmax(s, mask):
    s = jnp.where(mask, s.astype(jnp.float32), -jnp.inf)
    m = jnp.max(s, axis=-1, keepdims=True)
    m = jnp.where(jnp.isfinite(m), m, 0.0)
    p = jnp.exp(s - m)
    return p / jnp.maximum(jnp.sum(p, axis=-1, keepdims=True), 1e-30)


def swiglu(x, wg, wu, wd):
    return (jax.nn.silu(x @ wg) * (x @ wu)) @ wd


def moe_swiglu(x, w_router, b_router, w_gate, w_up, w_down):
    N = x.shape[0]
    NK = N * TOP_K
    logits = (x @ w_router).astype(jnp.float32) + b_router.astype(jnp.float32)
    top_val, top_idx = lax.top_k(logits, TOP_K)
    gate = jax.nn.softmax(top_val, axis=-1).astype(x.dtype)
    e_flat = top_idx.reshape(NK)
    tok_flat = jnp.repeat(jnp.arange(N, dtype=jnp.int32), TOP_K)
    g_flat = gate.reshape(NK)
    order = jnp.argsort(e_flat)
    e_s, tok_s, g_s = e_flat[order], tok_flat[order], g_flat[order]
    counts = jnp.bincount(e_flat, length=N_EXPERTS)
    starts = jnp.cumsum(counts) - counts
    padded = (counts + MOE_BLOCK - 1) // MOE_BLOCK * MOE_BLOCK
    pad_ends = jnp.cumsum(padded)
    pad_starts = pad_ends - padded
    dest = pad_starts[e_s] + jnp.arange(NK) - starts[e_s]
    n_blocks = -(-NK // MOE_BLOCK) + N_EXPERTS
    slot_tok = jnp.full((n_blocks * MOE_BLOCK,), N, jnp.int32).at[dest].set(tok_s)
    block_expert = jnp.minimum(jnp.searchsorted(pad_ends, jnp.arange(n_blocks) * MOE_BLOCK, side='right'), N_EXPERTS - 1)
    x_pad = jnp.concatenate([x, jnp.zeros((1, x.shape[1]), x.dtype)], axis=0)

    def expert_block(args):
        toks, e = args
        xb = x_pad[toks]
        return (jax.nn.silu(xb @ w_gate[e]) * (xb @ w_up[e])) @ w_down[e]

    y_slots = lax.map(expert_block, (slot_tok.reshape(n_blocks, MOE_BLOCK), block_expert))
    y_slots = y_slots.reshape(n_blocks * MOE_BLOCK, -1)
    return jax.ops.segment_sum(y_slots[dest] * g_s[:, None], tok_s, num_segments=N)


def conv_module(a, gate, prev, conv_w, conv_b, ln_g, ln_b):
    u = a * jax.nn.sigmoid(gate)
    up = jnp.concatenate([prev, u], axis=1)
    y = lax.conv_general_dilated(up, conv_w[:, None, :], window_strides=(1,), padding='VALID',
                                 dimension_numbers=('NWC', 'WIO', 'NWC'), feature_group_count=C_CONV) + conv_b
    y = jax.nn.silu(layernorm(y, ln_g, ln_b))
    return y, up[:, -(CONV_WIDTH - 1):]


def memory_kv(mem, g, w):
    B, M, _ = mem.shape
    kv = rmsnorm(mem, g) @ w
    k = kv[..., :MEM_WIDTH].reshape(B, M, MEM_HEADS, MEM_HEAD_DIM)
    v = kv[..., MEM_WIDTH:].reshape(B, M, MEM_HEADS, MEM_HEAD_DIM)
    return k, v


def memory_attend(q, k, v):
    B, T = q.shape[:2]
    s = jnp.einsum('bthd,bmhd->bhtm', q, k).astype(jnp.float32) * MEM_SCALE
    p = jax.nn.softmax(s, axis=-1).astype(v.dtype)
    return jnp.einsum('bhtm,bmhd->bthd', p, v).reshape(B, T, MEM_WIDTH)


def shared_kv(h, g, w):
    B, T, _ = h.shape
    kv = (rmsnorm(h, g) @ w).reshape(B, T, 2 * N_BRANCH, NSA_KV_HEADS, HEAD_DIM)
    return tuple(kv[:, :, i] for i in range(2 * N_BRANCH))


def chunk_proj(seq, w1):
    B, L = seq.shape[:2]
    ch = seq.reshape(B, L // CMP_STRIDE, CMP_STRIDE, NSA_KV_HEADS, HEAD_DIM)
    lo = jnp.einsum('bncgd,cdh->bngh', ch, w1[:CMP_STRIDE])
    hi = jnp.einsum('bncgd,cdh->bngh', ch, w1[CMP_STRIDE:])
    return lo, hi


def compress(segments, pe, w1, w2):
    parts = [chunk_proj(s, w1) for s in segments]
    lo = jnp.concatenate([p[0] for p in parts], axis=1)
    hi = jnp.concatenate([p[1] for p in parts], axis=1)
    pe_bias = jnp.einsum('cd,cdh->h', pe, w1)
    h = jax.nn.gelu(lo[:, :-1] + hi[:, 1:] + pe_bias)
    return jnp.einsum('bngh,hd->bngd', h, w2)


def cmp_attend(q, kc, vc, pos):
    ends = jnp.arange(kc.shape[1]) * CMP_STRIDE + CMP_BLOCK - 1
    mask = ends[None, :] <= pos[:, None]
    s = jnp.einsum('btgjd,bngd->bgjtn', q, kc).astype(jnp.float32) * ATTN_SCALE
    p = masked_softmax(s, mask)
    o = jnp.einsum('bgjtn,bngd->btgjd', p.astype(vc.dtype), vc)
    return o, p


def chunk_importance(p_c):
    half = 0.5 * jnp.sum(p_c, axis=2)
    return jnp.pad(half, ((0, 0), (0, 0), (0, 0), (0, 1))) + jnp.pad(half, ((0, 0), (0, 0), (0, 0), (1, 0)))


def sel_attend_prompt(q, sk, sv, idx, pos):
    B, T = q.shape[:2]
    n_sel = T // SEL_BLOCK
    k = idx.shape[-1]
    nq = T // SEL_Q_BLOCK
    kb = sk.reshape(B, n_sel, SEL_BLOCK, NSA_KV_HEADS, HEAD_DIM).transpose(0, 3, 1, 2, 4)
    vb = sv.reshape(B, n_sel, SEL_BLOCK, NSA_KV_HEADS, HEAD_DIM).transpose(0, 3, 1, 2, 4)
    bi = jnp.arange(B)[:, None, None, None]
    gi = jnp.arange(NSA_KV_HEADS)[None, :, None, None]
    qs = q.reshape(B, nq, SEL_Q_BLOCK, NSA_KV_HEADS, HEADS_PER_GROUP, HEAD_DIM).swapaxes(0, 1)
    ids = idx.reshape(B, NSA_KV_HEADS, nq, SEL_Q_BLOCK, k).transpose(2, 0, 1, 3, 4)
    ps = pos.reshape(nq, SEL_Q_BLOCK)

    def step(args):
        qb, ib, pb = args
        kg = kb[bi, gi, ib]
        vg = vb[bi, gi, ib]
        kpos = ib[..., None] * SEL_BLOCK + jnp.arange(SEL_BLOCK)
        mask = (kpos <= pb[:, None, None]) & (ib <= (pb // SEL_BLOCK)[:, None])[..., None]
        s = jnp.einsum('bqgjd,bgqnld->bgjqnl', qb, kg).astype(jnp.float32) * ATTN_SCALE
        p = masked_softmax(s.reshape(B, NSA_KV_HEADS, HEADS_PER_GROUP, SEL_Q_BLOCK, k * SEL_BLOCK),
                           mask.reshape(B, NSA_KV_HEADS, 1, SEL_Q_BLOCK, k * SEL_BLOCK))
        p = p.reshape(B, NSA_KV_HEADS, HEADS_PER_GROUP, SEL_Q_BLOCK, k, SEL_BLOCK).astype(vg.dtype)
        return jnp.einsum('bgjqnl,bgqnld->bqgjd', p, vg)

    o = lax.map(step, (qs, ids, ps))
    return o.swapaxes(0, 1).reshape(B, T, NSA_KV_HEADS, HEADS_PER_GROUP, HEAD_DIM)


def window_attend_prompt(q, wk, wv):
    B, T = q.shape[:2]
    nb = T // WIN_Q_BLOCK
    span = WIN_Q_BLOCK + WINDOW
    kp = jnp.pad(wk, ((0, 0), (WINDOW, 0), (0, 0), (0, 0)))
    vp = jnp.pad(wv, ((0, 0), (WINDOW, 0), (0, 0), (0, 0)))
    qrel = jnp.arange(WIN_Q_BLOCK)
    krel = jnp.arange(span) - WINDOW
    rel_ok = (krel[None, :] <= qrel[:, None]) & (krel[None, :] > qrel[:, None] - WINDOW)

    def step(n):
        start = n * WIN_Q_BLOCK
        qb = lax.dynamic_slice_in_dim(q, start, WIN_Q_BLOCK, axis=1)
        kb = lax.dynamic_slice_in_dim(kp, start, span, axis=1)
        vb = lax.dynamic_slice_in_dim(vp, start, span, axis=1)
        mask = rel_ok & (start + krel >= 0)[None, :]
        s = jnp.einsum('bqgjd,bkgd->bgjqk', qb, kb).astype(jnp.float32) * ATTN_SCALE
        p = masked_softmax(s, mask).astype(vb.dtype)
        return jnp.einsum('bgjqk,bkgd->bqgjd', p, vb)

    o = lax.map(step, jnp.arange(nb))
    return o.swapaxes(0, 1).reshape(B, T, NSA_KV_HEADS, HEADS_PER_GROUP, HEAD_DIM)


def nsa_prompt(q, gates, kvs, cmp_pe, cmp_w1, cmp_w2):
    ck, cv, sk, sv, wk, wv = kvs
    B, T = q.shape[:2]
    pos = jnp.arange(T)
    kc = compress((ck,), cmp_pe[0], cmp_w1[0], cmp_w2[0])
    vc = compress((cv,), cmp_pe[1], cmp_w1[1], cmp_w2[1])
    o_c, p_c = cmp_attend(q, kc, vc, pos)
    n_sel = T // SEL_BLOCK
    score = chunk_importance(p_c).reshape(B, NSA_KV_HEADS, T, n_sel, CHUNKS_PER_SEL).sum(-1)
    cur = (pos // SEL_BLOCK)[:, None]
    j = jnp.arange(n_sel)[None, :]
    forced = (j == 0) | (j == cur) | (j == cur - 1)
    score = jnp.where(forced, FORCE_SCORE, jnp.where(j <= cur, score, -jnp.inf))
    _, idx = lax.top_k(score, min(SEL_TOP_N, n_sel))
    o_s = sel_attend_prompt(q, sk, sv, idx, pos)
    o_w = window_attend_prompt(q, wk, wv)
    return gates[..., 0:1] * o_c + gates[..., 1:2] * o_s + gates[..., 2:3] * o_w


def sel_attend_sample(q, sk, sv, idx, pool_k, pool_v, page_table):
    B, S = q.shape[:2]
    k = idx.shape[-1]
    blocks_per_page = PAGE_SIZE // SEL_BLOCK
    bi = jnp.arange(B)[:, None, None, None]
    phys = page_table[bi, idx // blocks_per_page][..., None]
    rows = (idx % blocks_per_page)[..., None] * SEL_BLOCK + jnp.arange(SEL_BLOCK)
    gi = jnp.arange(NSA_KV_HEADS)[None, :, None, None, None]
    kg = pool_k[phys, rows, gi]
    vg = pool_v[phys, rows, gi]
    s_past = jnp.einsum('bsgjd,bgsnld->bgjsnl', q, kg).reshape(B, NSA_KV_HEADS, HEADS_PER_GROUP, S, k * SEL_BLOCK)
    s_new = jnp.einsum('bsgjd,brgd->bgjsr', q, sk)
    s = jnp.concatenate([s_past, s_new], axis=-1).astype(jnp.float32) * ATTN_SCALE
    causal = jnp.arange(S)[None, :] <= jnp.arange(S)[:, None]
    mask = jnp.concatenate([jnp.ones((S, k * SEL_BLOCK), bool), causal], axis=-1)
    p = masked_softmax(s, mask)
    p_past = p[..., :k * SEL_BLOCK].reshape(B, NSA_KV_HEADS, HEADS_PER_GROUP, S, k, SEL_BLOCK).astype(vg.dtype)
    p_new = p[..., k * SEL_BLOCK:].astype(sv.dtype)
    return jnp.einsum('bgjsnl,bgsnld->bsgjd', p_past, vg) + jnp.einsum('bgjsr,brgd->bsgjd', p_new, sv)


def window_attend_sample(q, wk, wv, buf_k, buf_v, pos):
    n_buf = buf_k.shape[1]
    kw = jnp.concatenate([buf_k, wk], axis=1)
    vw = jnp.concatenate([buf_v, wv], axis=1)
    kpos = PAST_LEN - n_buf + jnp.arange(kw.shape[1])
    mask = (kpos[None, :] <= pos[:, None]) & (kpos[None, :] > pos[:, None] - WINDOW)
    s = jnp.einsum('bsgjd,bkgd->bgjsk', q, kw).astype(jnp.float32) * ATTN_SCALE
    p = masked_softmax(s, mask).astype(vw.dtype)
    return jnp.einsum('bgjsk,bkgd->bsgjd', p, vw)


def nsa_sample(q, gates, kvs, cache_cmp_k, cache_cmp_v, cache_sel_k, cache_sel_v, page_table,
               state_win_k, state_win_v, cmp_pe, cmp_w1, cmp_w2):
    ck, cv, sk, sv, wk, wv = kvs
    B, S = q.shape[:2]
    pos = PAST_LEN + jnp.arange(S)
    n_full = (S // CMP_STRIDE) * CMP_STRIDE
    ck_past = cache_cmp_k[page_table].reshape(B, PAST_LEN, NSA_KV_HEADS, HEAD_DIM)
    cv_past = cache_cmp_v[page_table].reshape(B, PAST_LEN, NSA_KV_HEADS, HEAD_DIM)
    kc = compress((ck_past, ck[:, :n_full]), cmp_pe[0], cmp_w1[0], cmp_w2[0])
    vc = compress((cv_past, cv[:, :n_full]), cmp_pe[1], cmp_w1[1], cmp_w2[1])
    o_c, p_c = cmp_attend(q, kc, vc, pos)
    n_past_blk = PAST_LEN // SEL_BLOCK
    score = chunk_importance(p_c)[..., :n_past_blk * CHUNKS_PER_SEL]
    score = score.reshape(B, NSA_KV_HEADS, S, n_past_blk, CHUNKS_PER_SEL).sum(-1)
    j = jnp.arange(n_past_blk)
    score = jnp.where((j == 0) | (j == n_past_blk - 1), FORCE_SCORE, score)
    _, idx = lax.top_k(score, min(SEL_TOP_N - 1, n_past_blk))
    o_s = sel_attend_sample(q, sk, sv, idx, cache_sel_k, cache_sel_v, page_table)
    o_w = window_attend_sample(q, wk, wv, state_win_k, state_win_v, pos)
    return gates[..., 0:1] * o_c + gates[..., 1:2] * o_s + gates[..., 2:3] * o_w


def setup_inputs(seed: int = 0) -> dict:
    key = jax.random.key(seed)
    ks = iter(jax.random.split(key, 48))

    def nrm(shape, scale=1.0):
        return jax.random.normal(next(ks), shape, jnp.float32) * scale

    def gain(shape):
        return 1.0 + nrm(shape, 0.02)

    n_pages = PAST_LEN // PAGE_SIZE
    n_phys = (DEC_BATCH * n_pages * 5) // 4
    win_buf = min(WINDOW, PAST_LEN)
    G, HD, D = NSA_KV_HEADS, HEAD_DIM, D_MODEL
    inp = {}
    inp['x_prompt'] = nrm((BATCH, SEQ, D))
    inp['x_sample'] = nrm((DEC_BATCH, DEC_SEQ, D))
    inp['cache_mem_k'] = nrm((DEPTH, DEC_BATCH, MEM_LEN, MEM_HEADS, MEM_HEAD_DIM))
    inp['cache_mem_v'] = nrm((DEPTH, DEC_BATCH, MEM_LEN, MEM_HEADS, MEM_HEAD_DIM))
    inp['state_conv'] = nrm((N_A_LAYERS, DEC_BATCH, CONV_WIDTH - 1, C_CONV), 0.5)
    inp['cache_cmp_k'] = nrm((n_phys, PAGE_SIZE, G, HD))
    inp['cache_cmp_v'] = nrm((n_phys, PAGE_SIZE, G, HD))
    inp['cache_sel_k'] = nrm((n_phys, PAGE_SIZE, G, HD))
    inp['cache_sel_v'] = nrm((n_phys, PAGE_SIZE, G, HD))
    inp['state_win_k'] = nrm((DEC_BATCH, win_buf, G, HD))
    inp['state_win_v'] = nrm((DEC_BATCH, win_buf, G, HD))
    inp['page_table'] = jax.random.permutation(next(ks), n_phys)[:DEC_BATCH * n_pages].reshape(DEC_BATCH, n_pages).astype(jnp.int32)
    inp['mem_prompt'] = nrm((BATCH, MEM_LEN, D))
    inp['g_mix'] = gain((DEPTH, D))
    inp['g_ffn'] = gain((DEPTH, D))
    inp['g_mem'] = gain((DEPTH, D))
    inp['w_mem_kv'] = nrm((DEPTH, D, 2 * MEM_WIDTH), D ** -0.5)
    inp['w_in_a'] = nrm((N_A_LAYERS, D, A_IN), D ** -0.5)
    inp['conv_w'] = nrm((N_A_LAYERS, CONV_WIDTH, C_CONV), CONV_WIDTH ** -0.5)
    inp['conv_b'] = nrm((N_A_LAYERS, C_CONV), 0.02)
    inp['ln_g'] = gain((N_A_LAYERS, C_CONV))
    inp['ln_b'] = nrm((N_A_LAYERS, C_CONV), 0.02)
    inp['w_in_b'] = nrm((N_B_LAYERS, D, B_IN), D ** -0.5)
    inp['g_kv'] = gain((D,))
    inp['w_kv'] = nrm((D, 2 * N_BRANCH * G * HD), D ** -0.5)
    inp['cmp_pe'] = nrm((2, CMP_BLOCK, HD), 0.1)
    inp['cmp_w1'] = nrm((2, CMP_BLOCK, HD, CMP_HIDDEN), (CMP_BLOCK * HD) ** -0.5)
    inp['cmp_w2'] = nrm((2, CMP_HIDDEN, HD), CMP_HIDDEN ** -0.5)
    inp['w_out'] = nrm((DEPTH, MIX_WIDTH, D), MIX_WIDTH ** -0.5)
    inp['w_ff_gate'] = nrm((N_DENSE, D, D_FF), D ** -0.5)
    inp['w_ff_up'] = nrm((N_DENSE, D, D_FF), D ** -0.5)
    inp['w_ff_down'] = nrm((N_DENSE, D_FF, D), D_FF ** -0.5)
    inp['w_router'] = nrm((N_MOE, D, N_EXPERTS), D ** -0.5)
    inp['b_router'] = nrm((N_MOE, N_EXPERTS), 0.01)
    inp['w_e_gate'] = nrm((N_MOE, N_EXPERTS, D, D_EXPERT), D ** -0.5)
    inp['w_e_up'] = nrm((N_MOE, N_EXPERTS, D, D_EXPERT), D ** -0.5)
    inp['w_e_down'] = nrm((N_MOE, N_EXPERTS, D_EXPERT, D), D_EXPERT ** -0.5)
    inp['g_final'] = gain((D,))
    return inp


def reference(x_prompt, x_sample, cache_mem_k, cache_mem_v, state_conv, cache_cmp_k, cache_cmp_v,
              cache_sel_k, cache_sel_v, state_win_k, state_win_v, page_table, mem_prompt,
              g_mix, g_ffn, g_mem, w_mem_kv, w_in_a, conv_w, conv_b, ln_g, ln_b, w_in_b, g_kv, w_kv,
              cmp_pe, cmp_w1, cmp_w2, w_out, w_ff_gate, w_ff_up, w_ff_down, w_router, b_router,
              w_e_gate, w_e_up, w_e_down, g_final):

    def channel_mixer(h, l):
        B, T, _ = h.shape
        hn = rmsnorm(h, g_ffn[l])
        i = l // 2
        if l % 2 == 0:
            y = swiglu(hn, w_ff_gate[i], w_ff_up[i], w_ff_down[i])
        else:
            y = moe_swiglu(hn.reshape(B * T, D_MODEL), w_router[i], b_router[i],
                           w_e_gate[i], w_e_up[i], w_e_down[i]).reshape(B, T, D_MODEL)
        return h + y

    def run_group(x, conv_prev, mem_k, mem_v, nsa_branch):
        B, T, _ = x.shape
        h = x
        conv_tails = []
        shared = None
        for l in range(DEPTH):
            hn = rmsnorm(h, g_mix[l])
            if l < N_A_LAYERS:
                z = hn @ w_in_a[l]
                tok, tail = conv_module(z[..., :C_CONV], z[..., C_CONV:2 * C_CONV], conv_prev[l],
                                        conv_w[l], conv_b[l], ln_g[l], ln_b[l])
                conv_tails.append(tail)
                q_mem = z[..., 2 * C_CONV:]
            else:
                if shared is None:
                    shared = shared_kv(h, g_kv, w_kv)
                z = hn @ w_in_b[l - N_A_LAYERS]
                q = z[..., :NSA_WIDTH].reshape(B, T, NSA_KV_HEADS, HEADS_PER_GROUP, HEAD_DIM)
                gates = jax.nn.sigmoid(z[..., NSA_WIDTH:NSA_WIDTH + N_BRANCH * NSA_HEADS]).reshape(
                    B, T, NSA_KV_HEADS, HEADS_PER_GROUP, N_BRANCH)
                tok = nsa_branch(q, gates, shared).reshape(B, T, NSA_WIDTH)
                q_mem = z[..., NSA_WIDTH + N_BRANCH * NSA_HEADS:]
            mem_o = memory_attend(q_mem.reshape(B, T, MEM_HEADS, MEM_HEAD_DIM), mem_k[l], mem_v[l])
            h = h + jnp.concatenate([tok, mem_o], axis=-1) @ w_out[l]
            h = channel_mixer(h, l)
        return rmsnorm(h, g_final), jnp.stack(conv_tails), shared

    mem_kv_p = [memory_kv(mem_prompt, g_mem[l], w_mem_kv[l]) for l in range(DEPTH)]
    mem_k_p = jnp.stack([kv[0] for kv in mem_kv_p])
    mem_v_p = jnp.stack([kv[1] for kv in mem_kv_p])
    conv_zero = jnp.zeros((N_A_LAYERS, x_prompt.shape[0], CONV_WIDTH - 1, C_CONV), x_prompt.dtype)

    def nsa_p(q, g, kv):
        return nsa_prompt(q, g, kv, cmp_pe, cmp_w1, cmp_w2)

    def nsa_s(q, g, kv):
        return nsa_sample(q, g, kv, cache_cmp_k, cache_cmp_v, cache_sel_k, cache_sel_v, page_table,
                          state_win_k, state_win_v, cmp_pe, cmp_w1, cmp_w2)

    y_prompt, conv_p, kv_p = run_group(x_prompt, conv_zero, mem_k_p, mem_v_p, nsa_p)
    y_sample, conv_s, kv_s = run_group(x_sample, state_conv, cache_mem_k, cache_mem_v, nsa_s)

    keep_p = min(WINDOW, x_prompt.shape[1])
    win_k_p = kv_p[4][:, -keep_p:]
    win_v_p = kv_p[5][:, -keep_p:]
    keep_s = min(WINDOW, state_win_k.shape[1] + x_sample.shape[1])
    win_k_s = jnp.concatenate([state_win_k, kv_s[4]], axis=1)[:, -keep_s:]
    win_v_s = jnp.concatenate([state_win_v, kv_s[5]], axis=1)[:, -keep_s:]
    return (y_prompt, y_sample, mem_k_p, mem_v_p, conv_p, conv_s,
            kv_p[0], kv_p[1], kv_p[2], kv_p[3], kv_s[0], kv_s[1], kv_s[2], kv_s[3],
            win_k_p, win_v_p, win_k_s, win_v_s)
```

```python
import functools

import numpy as np
import jax
import jax.numpy as jnp
from jax import lax
from jax.experimental import pallas as pl
from jax.experimental.pallas import tpu as pltpu

F32 = jnp.float32
BF16 = jnp.bfloat16

D_MODEL = 1024
DEPTH = 2
C_CONV = 768
CONV_WIDTH = 31
CONV_TAIL = CONV_WIDTH - 1
MEM_LEN = 256
MEM_WIDTH = 256
HEAD_DIM = 64
NSA_HEADS = 12
NSA_KV_HEADS = 4
HEADS_PER_GROUP = 3
NSA_WIDTH = 768
N_BRANCH = 3
CMP_BLOCK = 32
CMP_STRIDE = 16
SEL_BLOCK = 64
SEL_TOP_N = 16
WINDOW = 512
FORCE_SCORE = 1e4
D_FF = 2816
N_EXPERTS = 8
D_EXPERT = 1408
PAGE_SIZE = 128
EPS = 1e-6
ATTN_SCALE = HEAD_DIM ** -0.5
MEM_SCALE = HEAD_DIM ** -0.5
KV_WIDTH = NSA_KV_HEADS * HEAD_DIM
GATE_PAD = 128
NEG_INF = float("-inf")

VMEM_LIMIT_BYTES = 56 * 1024 * 1024
NT_DIMS = (((1,), (1,)), ((), ()))
HIGHEST = lax.Precision.HIGHEST


def _params(*sem):
    return pltpu.CompilerParams(dimension_semantics=sem, vmem_limit_bytes=VMEM_LIMIT_BYTES)


def _unit_rms(x):
    return x * lax.rsqrt(jnp.mean(x * x, axis=-1, keepdims=True) + EPS)


def _mm(a, w):
    if w.dtype == BF16:
        return jnp.dot(a.astype(BF16), w, preferred_element_type=F32)
    return jnp.dot(a.astype(F32), w, precision=HIGHEST, preferred_element_type=F32)


def _split(x):
    hi = x.astype(BF16)
    return hi, (x - hi.astype(F32)).astype(BF16)


def _dot3(a, b, dims):
    ah, al = _split(a)
    bh, bl = _split(b)
    f = lambda x, y: lax.dot_general(x, y, dims, preferred_element_type=F32)
    return f(ah, bh) + f(ah, bl) + f(al, bh)


def _full(shape):
    n = len(shape)
    return pl.BlockSpec(shape, lambda *_: (0,) * n)


def _memkv_kernel(mem_ref, g_ref, wT_ref, k_ref, v_ref):
    xh = _unit_rms(mem_ref[0])
    for l in range(DEPTH):
        xn = (xh * g_ref[l:l + 1, :]).astype(BF16)
        kvT = lax.dot_general(wT_ref[l], xn, NT_DIMS, preferred_element_type=F32)
        k_ref[l, 0] = kvT[:MEM_WIDTH]
        v_ref[l, 0] = kvT[MEM_WIDTH:]


def _memkv(mem, g_mem, w_memT):
    B = mem.shape[0]
    out = jax.ShapeDtypeStruct((DEPTH, B, MEM_WIDTH, MEM_LEN), F32)
    return pl.pallas_call(
        _memkv_kernel,
        grid=(B,),
        in_specs=[pl.BlockSpec((1, MEM_LEN, D_MODEL), lambda b: (b, 0, 0)),
                  _full((DEPTH, D_MODEL)),
                  _full((DEPTH, 2 * MEM_WIDTH, D_MODEL))],
        out_specs=[pl.BlockSpec((DEPTH, 1, MEM_WIDTH, MEM_LEN), lambda b: (0, b, 0, 0))] * 2,
        out_shape=[out, out],
        compiler_params=_params("arbitrary"),
        name="mem_kv",
    )(mem, g_mem, w_memT)


def _in_a_kernel(h_ref, g_ref, w_ref, ag_ref, qm_ref):
    z = _mm(_unit_rms(h_ref[...]) * g_ref[...], w_ref[...])
    ag_ref[...] = z[:, :2 * C_CONV]
    qm_ref[...] = z[:, 2 * C_CONV:]


def _in_proj_a(h, g, w, tm):
    M = h.shape[0]
    return pl.pallas_call(
        _in_a_kernel,
        grid=(M // tm,),
        in_specs=[pl.BlockSpec((tm, D_MODEL), lambda i: (i, 0)),
                  _full((1, D_MODEL)),
                  _full((D_MODEL, 2 * C_CONV + MEM_WIDTH))],
        out_specs=[pl.BlockSpec((tm, 2 * C_CONV), lambda i: (i, 0)),
                   pl.BlockSpec((tm, MEM_WIDTH), lambda i: (i, 0))],
        out_shape=[jax.ShapeDtypeStruct((M, 2 * C_CONV), F32),
                   jax.ShapeDtypeStruct((M, MEM_WIDTH), F32)],
        compiler_params=_params("arbitrary"),
        name="in_proj_a",
    )(h, g, w)


HALO = 32


def _conv_kernel(ag_ref, prev_ref, w_ref, cb_ref, lg_ref, lb_ref, tok_ref, tail_ref, up_ref, y_ref, *, tt):
    @pl.when(pl.program_id(1) == 0)
    def _():
        up_ref[0:HALO - CONV_TAIL, :] = jnp.zeros((HALO - CONV_TAIL, C_CONV), F32)
        up_ref[HALO - CONV_TAIL:HALO, :] = prev_ref[0]

    u = ag_ref[0, :, :C_CONV] * jax.nn.sigmoid(ag_ref[0, :, C_CONV:])
    up_ref[HALO:HALO + tt, :] = u
    first = HALO - CONV_TAIL
    for c0 in range(0, C_CONV, 128):
        acc = jnp.broadcast_to(cb_ref[:, c0:c0 + 128], (tt, 128))
        for k in range(CONV_WIDTH):
            acc = acc + w_ref[k:k + 1, c0:c0 + 128] * up_ref[first + k:first + k + tt, c0:c0 + 128]
        y_ref[:, c0:c0 + 128] = acc
    y = y_ref[...]
    mu = jnp.mean(y, axis=-1, keepdims=True)
    yc = y - mu
    var = jnp.mean(yc * yc, axis=-1, keepdims=True)
    yn = yc * lax.rsqrt(var + EPS) * lg_ref[...] + lb_ref[...]
    tok_ref[0] = yn * jax.nn.sigmoid(yn)
    tail_ref[0] = up_ref[tt + HALO - CONV_TAIL:tt + HALO, :]
    nxt = up_ref[tt:tt + HALO, :]
    up_ref[0:HALO, :] = nxt


def _conv_module(ag, prev, conv_w, conv_b, ln_g, ln_b, tt):
    B, T, _ = ag.shape
    return pl.pallas_call(
        functools.partial(_conv_kernel, tt=tt),
        grid=(B, T // tt),
        in_specs=[pl.BlockSpec((1, tt, 2 * C_CONV), lambda b, t: (b, t, 0)),
                  pl.BlockSpec((1, CONV_TAIL, C_CONV), lambda b, t: (b, 0, 0)),
                  _full((CONV_WIDTH, C_CONV)), _full((1, C_CONV)), _full((1, C_CONV)), _full((1, C_CONV))],
        out_specs=[pl.BlockSpec((1, tt, C_CONV), lambda b, t: (b, t, 0)),
                   pl.BlockSpec((1, CONV_TAIL, C_CONV), lambda b, t: (b, 0, 0))],
        out_shape=[jax.ShapeDtypeStruct((B, T, C_CONV), F32),
                   jax.ShapeDtypeStruct((B, CONV_TAIL, C_CONV), F32)],
        scratch_shapes=[pltpu.VMEM((tt + HALO, C_CONV), F32), pltpu.VMEM((tt, C_CONV), F32)],
        compiler_params=_params("arbitrary", "arbitrary"),
        name="conv_module",
    )(ag, prev, conv_w, conv_b, ln_g, ln_b)


def _memattn_kernel(q_ref, kT_ref, vT_ref, o_ref, *, precise):
    q = q_ref[0] * MEM_SCALE
    for h in range(4):
        lo, hi = h * HEAD_DIM, (h + 1) * HEAD_DIM
        kT = kT_ref[0, 0, lo:hi, :]
        vT = vT_ref[0, 0, lo:hi, :]
        if precise:
            s = jnp.dot(q[:, lo:hi], kT, precision=HIGHEST, preferred_element_type=F32)
        else:
            s = jnp.dot(q[:, lo:hi].astype(BF16), kT.astype(BF16), preferred_element_type=F32)
        m = jnp.max(s, axis=-1, keepdims=True)
        p = jnp.exp(s - m)
        p = p / jnp.sum(p, axis=-1, keepdims=True)
        if precise:
            o_ref[0, :, lo:hi] = _dot3(p, vT, NT_DIMS)
        else:
            o_ref[0, :, lo:hi] = lax.dot_general(p.astype(BF16), vT.astype(BF16), NT_DIMS,
                                                 preferred_element_type=F32)


def _mem_attend(q, kT, vT, layer, tq, precise=False):
    B, T, _ = q.shape
    kv_spec = pl.BlockSpec((1, 1, MEM_WIDTH, MEM_LEN), lambda b, t: (layer, b, 0, 0))
    return pl.pallas_call(
        functools.partial(_memattn_kernel, precise=precise),
        grid=(B, T // tq),
        in_specs=[pl.BlockSpec((1, tq, MEM_WIDTH), lambda b, t: (b, t, 0)), kv_spec, kv_spec],
        out_specs=pl.BlockSpec((1, tq, MEM_WIDTH), lambda b, t: (b, t, 0)),
        out_shape=jax.ShapeDtypeStruct((B, T, MEM_WIDTH), F32),
        compiler_params=_params("arbitrary", "arbitrary"),
        name="mem_attend",
    )(q, kT, vT)


def _outproj_kernel(*refs, n_tok):
    h_ref = refs[0]
    tok_refs = refs[1:1 + n_tok]
    mem_ref, wt_ref, wm_ref, o_ref = refs[1 + n_tok:]
    tok = tok_refs[0][...]
    for r in tok_refs[1:]:
        tok = tok + r[...]
    o_ref[...] = h_ref[...] + _mm(tok, wt_ref[...]) + _mm(mem_ref[...], wm_ref[...])


def _out_proj(h, toks, mem_o, w_tok, w_mem, tm):
    M = h.shape[0]
    n_tok = len(toks)
    wide = toks[0].shape[1]
    return pl.pallas_call(
        functools.partial(_outproj_kernel, n_tok=n_tok),
        grid=(M // tm,),
        in_specs=([pl.BlockSpec((tm, D_MODEL), lambda i: (i, 0))]
                  + [pl.BlockSpec((tm, wide), lambda i: (i, 0))] * n_tok
                  + [pl.BlockSpec((tm, MEM_WIDTH), lambda i: (i, 0)),
                     _full((wide, D_MODEL)), _full((MEM_WIDTH, D_MODEL))]),
        out_specs=pl.BlockSpec((tm, D_MODEL), lambda i: (i, 0)),
        out_shape=jax.ShapeDtypeStruct((M, D_MODEL), F32),
        compiler_params=_params("arbitrary"),
        name="out_proj",
    )(h, *toks, mem_o, w_tok, w_mem)


def _ffn_kernel(h_ref, g_ref, wg_ref, wu_ref, wd_ref, o_ref, xn_ref, acc_ref):
    f = pl.program_id(1)

    @pl.when(f == 0)
    def _():
        xn_ref[...] = (_unit_rms(h_ref[...]) * g_ref[...]).astype(xn_ref.dtype)
        acc_ref[...] = jnp.zeros_like(acc_ref)

    xn = xn_ref[...]
    gg = _mm(xn, wg_ref[...])
    uu = _mm(xn, wu_ref[...])
    acc_ref[...] += _mm(gg * jax.nn.sigmoid(gg) * uu, wd_ref[...])

    @pl.when(f == pl.num_programs(1) - 1)
    def _():
        o_ref[...] = h_ref[...] + acc_ref[...]


def _ffn(h, g, wg, wu, wd, tm, tf):
    M = h.shape[0]
    return pl.pallas_call(
        _ffn_kernel,
        grid=(M // tm, D_FF // tf),
        in_specs=[pl.BlockSpec((tm, D_MODEL), lambda i, f: (i, 0)),
                  _full((1, D_MODEL)),
                  pl.BlockSpec((D_MODEL, tf), lambda i, f: (0, f)),
                  pl.BlockSpec((D_MODEL, tf), lambda i, f: (0, f)),
                  pl.BlockSpec((tf, D_MODEL), lambda i, f: (f, 0))],
        out_specs=pl.BlockSpec((tm, D_MODEL), lambda i, f: (i, 0)),
        out_shape=jax.ShapeDtypeStruct((M, D_MODEL), F32),
        scratch_shapes=[pltpu.VMEM((tm, D_MODEL), wg.dtype), pltpu.VMEM((tm, D_MODEL), F32)],
        compiler_params=_params("arbitrary", "arbitrary"),
        name="ffn_dense",
    )(h, g, wg, wu, wd)


IN_B_PAD = NSA_WIDTH + GATE_PAD + MEM_WIDTH


def _in_b_part(xh, gmix_ref, wb_ref, q_ref, gate_ref, qm_ref):
    z = _mm(xh * gmix_ref[...], wb_ref[...])
    q_ref[...] = z[:, :NSA_WIDTH]
    gate_ref[...] = jax.nn.sigmoid(z[:, NSA_WIDTH:NSA_WIDTH + GATE_PAD])
    qm_ref[...] = z[:, NSA_WIDTH + GATE_PAD:]


def _projb_prompt_kernel(h_ref, gkv_ref, gmix_ref, wkvT_ref, wcmp_ref, wb_ref,
                         k0, k1, k2, k3, k4, k5, crm_ref, q_ref, gate_ref, qm_ref):
    xh = _unit_rms(h_ref[...])
    xkv = (xh * gkv_ref[...]).astype(BF16)
    kvT = lax.dot_general(wkvT_ref[...], xkv, NT_DIMS, preferred_element_type=F32)
    for i, r in enumerate((k0, k1, k2, k3, k4, k5)):
        r[0] = kvT[i * KV_WIDTH:(i + 1) * KV_WIDTH]
    crm_ref[...] = jnp.dot(xkv, wcmp_ref[...], preferred_element_type=F32)
    _in_b_part(xh, gmix_ref, wb_ref, q_ref, gate_ref, qm_ref)


def _proj_b_prompt(h, B, T, g_kv, g_mix, w_kvT, w_cmp, w_b, tm):
    M = h.shape[0]
    per_b = T // tm
    kt_spec = pl.BlockSpec((1, KV_WIDTH, tm), lambda i: (i // per_b, 0, i % per_b))
    row = lambda w: pl.BlockSpec((tm, w), lambda i: (i, 0))
    return pl.pallas_call(
        _projb_prompt_kernel,
        grid=(M // tm,),
        in_specs=[row(D_MODEL), _full((1, D_MODEL)), _full((1, D_MODEL)),
                  _full((2 * N_BRANCH * KV_WIDTH, D_MODEL)), _full((D_MODEL, 2 * KV_WIDTH)),
                  _full((D_MODEL, IN_B_PAD))],
        out_specs=[kt_spec] * 6 + [row(2 * KV_WIDTH), row(NSA_WIDTH), row(GATE_PAD), row(MEM_WIDTH)],
        out_shape=([jax.ShapeDtypeStruct((B, KV_WIDTH, T), F32)] * 6
                   + [jax.ShapeDtypeStruct((M, 2 * KV_WIDTH), F32),
                      jax.ShapeDtypeStruct((M, NSA_WIDTH), F32),
                      jax.ShapeDtypeStruct((M, GATE_PAD), F32),
                      jax.ShapeDtypeStruct((M, MEM_WIDTH), F32)]),
        compiler_params=_params("arbitrary"),
        name="proj_b_prompt",
    )(h, g_kv, g_mix, w_kvT, w_cmp, w_b)


def _projb_sample_kernel(h_ref, gkv_ref, gmix_ref, wkv_ref, wb_ref, kv_ref, q_ref, gate_ref, qm_ref):
    xh = _unit_rms(h_ref[...])
    kv_ref[...] = _mm(xh * gkv_ref[...], wkv_ref[...])
    _in_b_part(xh, gmix_ref, wb_ref, q_ref, gate_ref, qm_ref)


def _proj_b_sample(h, g_kv, g_mix, w_kv, w_b):
    M = h.shape[0]
    wide = 2 * N_BRANCH * KV_WIDTH
    return pl.pallas_call(
        _projb_sample_kernel,
        grid=(1,),
        in_specs=[_full((M, D_MODEL)), _full((1, D_MODEL)), _full((1, D_MODEL)),
                  _full((D_MODEL, wide)), _full((D_MODEL, IN_B_PAD))],
        out_specs=[_full((M, wide)), _full((M, NSA_WIDTH)), _full((M, GATE_PAD)), _full((M, MEM_WIDTH))],
        out_shape=[jax.ShapeDtypeStruct((M, wide), F32), jax.ShapeDtypeStruct((M, NSA_WIDTH), F32),
                   jax.ShapeDtypeStruct((M, GATE_PAD), F32), jax.ShapeDtypeStruct((M, MEM_WIDTH), F32)],
        compiler_params=_params("arbitrary"),
        name="proj_b_sample",
    )(h, g_kv, g_mix, w_kv, w_b)


CHUNK_IN = CMP_STRIDE * KV_WIDTH
CHUNK_OUT = 2 * KV_WIDTH


def _chunk_mm_kernel(xk_ref, xv_ref, wk_ref, wv_ref, ok_ref, ov_ref):
    ok_ref[...] = jnp.dot(xk_ref[...].astype(BF16), wk_ref[...], preferred_element_type=F32)
    ov_ref[...] = jnp.dot(xv_ref[...].astype(BF16), wv_ref[...], preferred_element_type=F32)


def _chunk_proj_prompt(xk, xv, wk, wv, tr):
    R = xk.shape[0]
    row = lambda w: pl.BlockSpec((tr, w), lambda i: (i, 0))
    return pl.pallas_call(
        _chunk_mm_kernel,
        grid=(R // tr,),
        in_specs=[row(CHUNK_IN), row(CHUNK_IN), _full((CHUNK_IN, CHUNK_OUT)), _full((CHUNK_IN, CHUNK_OUT))],
        out_specs=[row(CHUNK_OUT), row(CHUNK_OUT)],
        out_shape=[jax.ShapeDtypeStruct((R, CHUNK_OUT), F32)] * 2,
        compiler_params=_params("arbitrary"),
        name="chunk_proj_prompt",
    )(xk, xv, wk, wv)


PAGES_PER_STEP = 16
CHUNKS_PER_PAGE = PAGE_SIZE // CMP_STRIDE


def _chunk_gather_kernel(pt_ref, *refs):
    P = PAGES_PER_STEP
    k_pages = refs[:P]
    v_pages = refs[P:2 * P]
    wk_ref, wv_ref, ok_ref, ov_ref, xa_ref, xb_ref = refs[2 * P:]
    rows = P * CHUNKS_PER_PAGE
    for pages, w_ref, o_ref in ((k_pages, wk_ref, ok_ref), (v_pages, wv_ref, ov_ref)):
        for j in range(P):
            xt = pages[j][0].T
            xa_ref[j * PAGE_SIZE:(j + 1) * PAGE_SIZE, :] = xt[:, :128]
            xb_ref[j * PAGE_SIZE:(j + 1) * PAGE_SIZE, :] = xt[:, 128:]
        acc = jnp.zeros((rows, CHUNK_OUT), F32)
        for c in range(CMP_STRIDE):
            xc = jnp.concatenate([xa_ref[pl.ds(c, rows, stride=CMP_STRIDE), :],
                                  xb_ref[pl.ds(c, rows, stride=CMP_STRIDE), :]], axis=-1).astype(BF16)
            acc = acc + jnp.dot(xc, w_ref[c], preferred_element_type=F32)
        o_ref[0] = acc


def _chunk_proj_sample(page_table, pool_k, pool_v, wk, wv):
    B, n_pages = page_table.shape
    P = PAGES_PER_STEP
    steps = n_pages // P
    rows = P * CHUNKS_PER_PAGE

    def page_spec(j):
        return pl.BlockSpec((1, KV_WIDTH, PAGE_SIZE), lambda b, i, pt: (pt[b * n_pages + i * P + j], 0, 0))

    w_spec = pl.BlockSpec((CMP_STRIDE, KV_WIDTH, CHUNK_OUT), lambda b, i, pt: (0, 0, 0))
    o_spec = pl.BlockSpec((1, rows, CHUNK_OUT), lambda b, i, pt: (b, i, 0))
    out = jax.ShapeDtypeStruct((B, n_pages * CHUNKS_PER_PAGE, CHUNK_OUT), F32)
    return pl.pallas_call(
        _chunk_gather_kernel,
        grid_spec=pltpu.PrefetchScalarGridSpec(
            num_scalar_prefetch=1,
            grid=(B, steps),
            in_specs=[page_spec(j) for j in range(P)] * 2 + [w_spec, w_spec],
            out_specs=[o_spec, o_spec],
            scratch_shapes=[pltpu.VMEM((P * PAGE_SIZE, 128), F32)] * 2),
        out_shape=[out, out],
        compiler_params=_params("arbitrary", "arbitrary"),
        name="chunk_proj_sample",
    )(page_table.reshape(-1), *([pool_k] * P), *([pool_v] * P), wk, wv)


def _cmp_finish_kernel(lk_ref, lv_ref, pe_ref, w1_ref, w2_ref, kcT_ref, vc_ref, *, nch):
    for which, l_ref in enumerate((lk_ref, lv_ref)):
        lh = l_ref[0]
        lo = lh[:, :KV_WIDTH]
        hi_next = pltpu.roll(lh[:, KV_WIDTH:], nch - 1, axis=0)
        peb = jnp.dot(pe_ref[which], w1_ref[which], precision=HIGHEST, preferred_element_type=F32)
        peb = jnp.concatenate([peb] * NSA_KV_HEADS, axis=-1)
        hid = jax.nn.gelu(lo + hi_next + peb)
        out = jnp.dot(hid.astype(BF16), w2_ref[which], preferred_element_type=F32)
        if which == 0:
            kcT_ref[0] = out.T
        else:
            for g in range(NSA_KV_HEADS):
                vc_ref[0, g] = out[:, g * HEAD_DIM:(g + 1) * HEAD_DIM]


def _cmp_finish(lohi_k, lohi_v, pe_flat, w1_flat, w2_bd):
    B, nch, _ = lohi_k.shape
    l_spec = pl.BlockSpec((1, nch, CHUNK_OUT), lambda b: (b, 0, 0))
    return pl.pallas_call(
        functools.partial(_cmp_finish_kernel, nch=nch),
        grid=(B,),
        in_specs=[l_spec, l_spec, _full((2, 1, CMP_BLOCK * HEAD_DIM)), _full((2, CMP_BLOCK * HEAD_DIM, HEAD_DIM)),
                  _full((2, KV_WIDTH, KV_WIDTH))],
        out_specs=[pl.BlockSpec((1, KV_WIDTH, nch), lambda b: (b, 0, 0)),
                   pl.BlockSpec((1, NSA_KV_HEADS, nch, HEAD_DIM), lambda b: (b, 0, 0, 0))],
        out_shape=[jax.ShapeDtypeStruct((B, KV_WIDTH, nch), F32),
                   jax.ShapeDtypeStruct((B, NSA_KV_HEADS, nch, HEAD_DIM), F32)],
        compiler_params=_params("arbitrary"),
        name="cmp_finish",
    )(lohi_k, lohi_v, pe_flat, w1_flat, w2_bd)


def _stack_heads(q, g):
    parts = [q[:, (g * HEADS_PER_GROUP + j) * HEAD_DIM:(g * HEADS_PER_GROUP + j + 1) * HEAD_DIM]
             for j in range(HEADS_PER_GROUP)]
    return (jnp.concatenate(parts, axis=0) * ATTN_SCALE).astype(BF16)


def _gate_col(gates, g, j, branch):
    c = (g * HEADS_PER_GROUP + j) * N_BRANCH + branch
    return gates[:, c:c + 1]


def _cmp_prompt_kernel(q_ref, gate_ref, kcT_ref, vc_ref, spread_ref, oc_ref, sel_ref, *, tq, n_sel):
    t0 = pl.program_id(1) * tq
    q = q_ref[0]
    gates = gate_ref[0]
    ncp = kcT_ref.shape[-1]
    pos = t0 + lax.broadcasted_iota(jnp.int32, (tq, 1), 0)
    blk = lax.broadcasted_iota(jnp.int32, (tq, ncp), 1)
    vis = (blk * CMP_STRIDE + CMP_BLOCK - 1 <= pos) & (blk < ncp - 1)
    bias = jnp.where(vis, 0.0, NEG_INF)
    bias3 = jnp.concatenate([bias] * HEADS_PER_GROUP, axis=0)
    cur = pos // SEL_BLOCK
    jj = lax.broadcasted_iota(jnp.int32, (tq, n_sel), 1)
    forced = (jj == 0) | (jj == cur) | (jj == cur - 1)
    for g in range(NSA_KV_HEADS):
        qg = _stack_heads(q, g)
        s = jnp.dot(qg, kcT_ref[0, g].astype(BF16), preferred_element_type=F32) + bias3
        m = jnp.max(s, axis=-1, keepdims=True)
        m = jnp.where(m > NEG_INF, m, 0.0)
        p = jnp.exp(s - m)
        p = p / jnp.maximum(jnp.sum(p, axis=-1, keepdims=True), 1e-30)
        o = jnp.dot(p.astype(BF16), vc_ref[0, g].astype(BF16), preferred_element_type=F32)
        for j in range(HEADS_PER_GROUP):
            c0 = (g * HEADS_PER_GROUP + j) * HEAD_DIM
            oc_ref[0, :, c0:c0 + HEAD_DIM] = o[j * tq:(j + 1) * tq] * _gate_col(gates, g, j, 0)
        p3 = p[0:tq] + p[tq:2 * tq] + p[2 * tq:3 * tq]
        score = jnp.dot(p3, spread_ref[...], precision=HIGHEST, preferred_element_type=F32)
        sc = jnp.where(forced, FORCE_SCORE, jnp.where(jj <= cur, score, NEG_INF))
        rank = jnp.zeros((tq, n_sel), F32)
        for i in range(n_sel):
            si = sc[:, i:i + 1]
            rank = rank + jnp.where((si > sc) | ((si == sc) & (i < jj)), 1.0, 0.0)
        sel = (rank < SEL_TOP_N) & (jj <= cur)
        sel_ref[0, :, g * n_sel:(g + 1) * n_sel] = jnp.where(sel, 1.0, 0.0)


def _cmp_attend_prompt(q, gates, kcT, vc, spread, tq):
    B, T, _ = q.shape
    n_sel = T // SEL_BLOCK
    ncp = kcT.shape[-1]
    return pl.pallas_call(
        functools.partial(_cmp_prompt_kernel, tq=tq, n_sel=n_sel),
        grid=(B, T // tq),
        in_specs=[pl.BlockSpec((1, tq, NSA_WIDTH), lambda b, t: (b, t, 0)),
                  pl.BlockSpec((1, tq, GATE_PAD), lambda b, t: (b, t, 0)),
                  pl.BlockSpec((1, NSA_KV_HEADS, HEAD_DIM, ncp), lambda b, t: (b, 0, 0, 0)),
                  pl.BlockSpec((1, NSA_KV_HEADS, ncp, HEAD_DIM), lambda b, t: (b, 0, 0, 0)),
                  _full((ncp, n_sel))],
        out_specs=[pl.BlockSpec((1, tq, NSA_WIDTH), lambda b, t: (b, t, 0)),
                   pl.BlockSpec((1, tq, NSA_KV_HEADS * n_sel), lambda b, t: (b, t, 0))],
        out_shape=[jax.ShapeDtypeStruct((B, T, NSA_WIDTH), F32),
                   jax.ShapeDtypeStruct((B, T, NSA_KV_HEADS * n_sel), F32)],
        compiler_params=_params("arbitrary", "arbitrary"),
        name="cmp_attend_prompt",
    )(q, gates, kcT, vc, spread)


def _flash_prompt_kernel(*refs, tq, tk, mode):
    if mode == "sel":
        q_ref, gate_ref, sel_ref, kT_ref, vT_ref, o_ref, m_ref, l_ref, acc_ref = refs
    else:
        q_ref, gate_ref, kT_ref, vT_ref, o_ref, m_ref, l_ref, acc_ref = refs
    qi = pl.program_id(1)
    t0 = qi * tq
    q = q_ref[0]
    gates = gate_ref[0]
    branch = 1 if mode == "sel" else 2
    qpos = t0 + lax.broadcasted_iota(jnp.int32, (tq, tk), 0)
    lane = lax.broadcasted_iota(jnp.int32, (tq, tk), 1)
    n_sel = sel_ref.shape[-1] // NSA_KV_HEADS if mode == "sel" else 0
    n_tiles = qi + 1 if mode == "sel" else jnp.minimum(qi, WINDOW // tk) + 1
    rows = HEADS_PER_GROUP * tq

    for g in range(NSA_KV_HEADS):
        qg = _stack_heads(q, g)
        m_ref[...] = jnp.full((rows, 1), NEG_INF, F32)
        l_ref[...] = jnp.zeros((rows, 1), F32)
        acc_ref[...] = jnp.zeros((rows, HEAD_DIM), F32)
        if mode == "sel":
            selg = sel_ref[0, :, g * n_sel:(g + 1) * n_sel].astype(BF16)

        def body(it, carry):
            kt = qi - it
            off = pl.multiple_of(kt * tk, tk)
            kT = kT_ref[0, g * HEAD_DIM:(g + 1) * HEAD_DIM, pl.ds(off, tk)].astype(BF16)
            vT = vT_ref[0, g * HEAD_DIM:(g + 1) * HEAD_DIM, pl.ds(off, tk)].astype(BF16)
            kpos = off + lane
            if mode == "sel":
                jrow = lax.broadcasted_iota(jnp.int32, (n_sel, tk), 0)
                jcol = kt * (tk // SEL_BLOCK) + lax.broadcasted_iota(jnp.int32, (n_sel, tk), 1) // SEL_BLOCK
                expand = jnp.where(jrow == jcol, 1.0, 0.0).astype(BF16)
                chosen = jnp.dot(selg, expand, preferred_element_type=F32) > 0.5
                ok = chosen & (kpos <= qpos)
            else:
                ok = (kpos <= qpos) & (kpos > qpos - WINDOW)
            bias = jnp.where(ok, 0.0, NEG_INF)
            s = jnp.dot(qg, kT, preferred_element_type=F32) + jnp.concatenate([bias] * HEADS_PER_GROUP, axis=0)
            m_old = m_ref[...]
            m_new = jnp.maximum(m_old, jnp.max(s, axis=-1, keepdims=True))
            alpha = jnp.exp(m_old - m_new)
            p = jnp.exp(s - m_new)
            l_ref[...] = alpha * l_ref[...] + jnp.sum(p, axis=-1, keepdims=True)
            acc_ref[...] = alpha * acc_ref[...] + lax.dot_general(p.astype(BF16), vT, NT_DIMS,
                                                                  preferred_element_type=F32)
            m_ref[...] = m_new
            return carry

        lax.fori_loop(0, n_tiles, body, 0)
        o = acc_ref[...] / l_ref[...]
        for j in range(HEADS_PER_GROUP):
            c0 = (g * HEADS_PER_GROUP + j) * HEAD_DIM
            o_ref[0, :, c0:c0 + HEAD_DIM] = o[j * tq:(j + 1) * tq] * _gate_col(gates, g, j, branch)


def _flash_prompt(q, gates, sel, kT, vT, tq, tk, mode):
    B, T, _ = q.shape
    rows = HEADS_PER_GROUP * tq
    qspec = lambda w: pl.BlockSpec((1, tq, w), lambda b, t: (b, t, 0))
    kvspec = pl.BlockSpec((1, KV_WIDTH, T), lambda b, t: (b, 0, 0))
    ins = [q, gates] + ([sel] if mode == "sel" else []) + [kT, vT]
    in_specs = [qspec(NSA_WIDTH), qspec(GATE_PAD)] + ([qspec(sel.shape[-1])] if mode == "sel" else []) + [kvspec, kvspec]
    return pl.pallas_call(
        functools.partial(_flash_prompt_kernel, tq=tq, tk=tk, mode=mode),
        grid=(B, T // tq),
        in_specs=in_specs,
        out_specs=qspec(NSA_WIDTH),
        out_shape=jax.ShapeDtypeStruct((B, T, NSA_WIDTH), F32),
        scratch_shapes=[pltpu.VMEM((rows, 1), F32), pltpu.VMEM((rows, 1), F32), pltpu.VMEM((rows, HEAD_DIM), F32)],
        compiler_params=_params("arbitrary", "arbitrary"),
        name="flash_" + mode,
    )(*ins)


Q_SLOTS = 8
S_NEW = 4
Q_ROWS = S_NEW * Q_SLOTS
NEW_PAD = 128


def _masked_softmax(s, ok):
    s = jnp.where(ok, s, NEG_INF)
    m = jnp.max(s, axis=-1, keepdims=True)
    m = jnp.where(m > NEG_INF, m, 0.0)
    p = jnp.exp(s - m)
    return p / jnp.maximum(jnp.sum(p, axis=-1, keepdims=True), 1e-30)


def _cmp_sample_kernel(q_ref, kcT_ref, vc_ref, spread_ref, oc_ref, idx_ref, *, n_blk, n_pick):
    qg = q_ref[0, 0] * ATTN_SCALE
    ncp = kcT_ref.shape[-1]
    s = jnp.dot(qg, kcT_ref[0, 0], precision=HIGHEST, preferred_element_type=F32)
    blk = lax.broadcasted_iota(jnp.int32, (Q_ROWS, ncp), 1)
    p = _masked_softmax(s, blk < ncp - 1)
    oc_ref[0, 0] = jnp.dot(p, vc_ref[0, 0], precision=HIGHEST, preferred_element_type=F32)
    slot = lax.broadcasted_iota(jnp.int32, (Q_ROWS, ncp), 0) % Q_SLOTS
    p = jnp.where(slot < HEADS_PER_GROUP, p, 0.0)
    p3 = jnp.sum(p.reshape(S_NEW, Q_SLOTS, ncp), axis=1)
    score = jnp.dot(p3, spread_ref[...], precision=HIGHEST, preferred_element_type=F32)
    jj = lax.broadcasted_iota(jnp.int32, (S_NEW, n_blk), 1)
    sc = jnp.where((jj == 0) | (jj == n_blk - 1), FORCE_SCORE, score)
    lane = lax.broadcasted_iota(jnp.int32, (S_NEW, 128), 1)
    picks = jnp.zeros((S_NEW, 128), jnp.int32)
    for n in range(n_pick):
        m = jnp.max(sc, axis=-1, keepdims=True)
        first = jnp.min(jnp.where(sc == m, jj, n_blk), axis=-1, keepdims=True)
        picks = jnp.where(lane == n, first, picks)
        sc = jnp.where(jj == first, NEG_INF, sc)
    idx_ref[0, 0] = picks


def _cmp_attend_sample(q32, kcT, vc, spread, n_pick):
    B = q32.shape[0]
    ncp = kcT.shape[-1]
    n_blk = spread.shape[1]
    bg = lambda *tail: pl.BlockSpec((1, 1) + tail, lambda b, g: (b, g, 0, 0))
    return pl.pallas_call(
        functools.partial(_cmp_sample_kernel, n_blk=n_blk, n_pick=n_pick),
        grid=(B, NSA_KV_HEADS),
        in_specs=[bg(Q_ROWS, HEAD_DIM), bg(HEAD_DIM, ncp), bg(ncp, HEAD_DIM), _full((ncp, n_blk))],
        out_specs=[bg(Q_ROWS, HEAD_DIM), bg(S_NEW, 128)],
        out_shape=[jax.ShapeDtypeStruct((B, NSA_KV_HEADS, Q_ROWS, HEAD_DIM), F32),
                   jax.ShapeDtypeStruct((B, NSA_KV_HEADS, S_NEW, 128), jnp.int32)],
        compiler_params=_params("arbitrary", "arbitrary"),
        name="cmp_attend_sample",
    )(q32, kcT, vc, spread)


def _sel_sample_kernel(idx_ref, pt_ref, *refs, n_pick):
    del pt_ref
    q_ref = refs[0]
    k_tiles = refs[1:1 + n_pick]
    v_tiles = refs[1 + n_pick:1 + 2 * n_pick]
    knT_ref, vnT_ref, o_ref = refs[1 + 2 * n_pick:]
    b, g, s_new = pl.program_id(0), pl.program_id(1), pl.program_id(2)
    q8 = q_ref[0, 0, 0] * ATTN_SCALE
    lane = lax.broadcasted_iota(jnp.int32, (Q_SLOTS, PAGE_SIZE), 1)
    base = ((b * NSA_KV_HEADS + g) * S_NEW + s_new) * SEL_TOP_N
    scores = []
    for n in range(n_pick):
        half = idx_ref[base + n] % (PAGE_SIZE // SEL_BLOCK)
        sn = jnp.dot(q8, k_tiles[n][0, 0], precision=HIGHEST, preferred_element_type=F32)
        scores.append(jnp.where(lane // SEL_BLOCK == half, sn, NEG_INF))
    sn = jnp.dot(q8, knT_ref[0, 0], precision=HIGHEST, preferred_element_type=F32)
    scores.append(jnp.where(lane <= s_new, sn, NEG_INF))
    s = jnp.concatenate(scores, axis=-1)
    m = jnp.max(s, axis=-1, keepdims=True)
    p = jnp.exp(s - m)
    p = p / jnp.sum(p, axis=-1, keepdims=True)
    o = jnp.zeros((Q_SLOTS, HEAD_DIM), F32)
    for n in range(n_pick + 1):
        vT = v_tiles[n][0, 0] if n < n_pick else vnT_ref[0, 0]
        o = o + _dot3(p[:, n * PAGE_SIZE:(n + 1) * PAGE_SIZE], vT, NT_DIMS)
    o_ref[0, 0, 0] = o


def _sel_attend_sample(idx_flat, pt_flat, q8, pool_k, pool_v, knT, vnT, n_pages, n_pick):
    B = q8.shape[0]
    per_page = PAGE_SIZE // SEL_BLOCK

    def tile_spec(n):
        def imap(b, g, s, idx, pt):
            blk = idx[((b * NSA_KV_HEADS + g) * S_NEW + s) * SEL_TOP_N + n]
            return (pt[b * n_pages + blk // per_page], g, 0, 0)
        return pl.BlockSpec((1, 1, HEAD_DIM, PAGE_SIZE), imap)

    q_spec = pl.BlockSpec((1, 1, 1, Q_SLOTS, HEAD_DIM), lambda b, g, s, idx, pt: (b, g, s, 0, 0))
    new_spec = pl.BlockSpec((1, 1, HEAD_DIM, NEW_PAD), lambda b, g, s, idx, pt: (b, g, 0, 0))
    return pl.pallas_call(
        functools.partial(_sel_sample_kernel, n_pick=n_pick),
        grid_spec=pltpu.PrefetchScalarGridSpec(
            num_scalar_prefetch=2,
            grid=(B, NSA_KV_HEADS, S_NEW),
            in_specs=[q_spec] + [tile_spec(n) for n in range(n_pick)] * 2 + [new_spec, new_spec],
            out_specs=q_spec),
        out_shape=jax.ShapeDtypeStruct((B, NSA_KV_HEADS, S_NEW, Q_SLOTS, HEAD_DIM), F32),
        compiler_params=_params("arbitrary", "arbitrary", "arbitrary"),
        name="sel_attend_sample",
    )(idx_flat, pt_flat, q8, *([pool_k] * n_pick), *([pool_v] * n_pick), knT, vnT)


def _win_sample_kernel(q_ref, kT_ref, vT_ref, knT_ref, vnT_ref, o_ref, *, n_buf):
    qg = q_ref[0, 0] * ATTN_SCALE
    s_of_row = lax.broadcasted_iota(jnp.int32, (Q_ROWS, 1), 0) // Q_SLOTS
    i_buf = lax.broadcasted_iota(jnp.int32, (Q_ROWS, n_buf), 1)
    i_new = lax.broadcasted_iota(jnp.int32, (Q_ROWS, NEW_PAD), 1)
    ok_buf = (i_buf - n_buf) > (s_of_row - WINDOW)
    ok_new = (i_new <= s_of_row) & (i_new < S_NEW)
    s_buf = jnp.dot(qg, kT_ref[0, 0], precision=HIGHEST, preferred_element_type=F32)
    s_nw = jnp.dot(qg, knT_ref[0, 0], precision=HIGHEST, preferred_element_type=F32)
    s = jnp.concatenate([jnp.where(ok_buf, s_buf, NEG_INF), jnp.where(ok_new, s_nw, NEG_INF)], axis=-1)
    m = jnp.max(s, axis=-1, keepdims=True)
    p = jnp.exp(s - m)
    p = p / jnp.sum(p, axis=-1, keepdims=True)
    o_ref[0, 0] = _dot3(p[:, :n_buf], vT_ref[0, 0], NT_DIMS) + _dot3(p[:, n_buf:], vnT_ref[0, 0], NT_DIMS)


def _win_attend_sample(q32, bufkT, bufvT, knT, vnT):
    B = q32.shape[0]
    n_buf = bufkT.shape[-1]
    bg = lambda *tail: pl.BlockSpec((1, 1) + tail, lambda b, g: (b, g, 0, 0))
    return pl.pallas_call(
        functools.partial(_win_sample_kernel, n_buf=n_buf),
        grid=(B, NSA_KV_HEADS),
        in_specs=[bg(Q_ROWS, HEAD_DIM), bg(HEAD_DIM, n_buf), bg(HEAD_DIM, n_buf), bg(HEAD_DIM, NEW_PAD),
                  bg(HEAD_DIM, NEW_PAD)],
        out_specs=bg(Q_ROWS, HEAD_DIM),
        out_shape=jax.ShapeDtypeStruct((B, NSA_KV_HEADS, Q_ROWS, HEAD_DIM), F32),
        compiler_params=_params("arbitrary", "arbitrary"),
        name="win_attend_sample",
    )(q32, bufkT, bufvT, knT, vnT)


def _gate_mix_kernel(gate_ref, oc_ref, os_ref, ow_ref, e_ref, o_ref):
    gates = gate_ref[...]
    acc = None
    for br, r in enumerate((oc_ref, os_ref, ow_ref)):
        wide = jnp.dot(gates, e_ref[br], precision=HIGHEST, preferred_element_type=F32)
        term = wide * r[...]
        acc = term if acc is None else acc + term
    o_ref[...] = acc


def _gate_mix(gates, oc, os_, ow, expand):
    M = gates.shape[0]
    return pl.pallas_call(
        _gate_mix_kernel,
        grid=(1,),
        in_specs=[_full((M, GATE_PAD))] + [_full((M, NSA_WIDTH))] * 3 + [_full((N_BRANCH, GATE_PAD, NSA_WIDTH))],
        out_specs=_full((M, NSA_WIDTH)),
        out_shape=jax.ShapeDtypeStruct((M, NSA_WIDTH), F32),
        compiler_params=_params("arbitrary"),
        name="gate_mix",
    )(gates, oc, os_, ow, expand)


def _win_shift_kernel(st_ref, new_ref, o_ref, *, n_buf):
    rolled = pltpu.roll(st_ref[0], n_buf - S_NEW, axis=1)
    o_ref[0, :, :n_buf - NEW_PAD] = rolled[:, :n_buf - NEW_PAD]
    lane = lax.broadcasted_iota(jnp.int32, (KV_WIDTH, NEW_PAD), 1)
    o_ref[0, :, n_buf - NEW_PAD:] = jnp.where(lane >= NEW_PAD - S_NEW, new_ref[0], rolled[:, n_buf - NEW_PAD:])


def _win_shift(stT, new_tail):
    B, _, n_buf = stT.shape
    return pl.pallas_call(
        functools.partial(_win_shift_kernel, n_buf=n_buf),
        grid=(B,),
        in_specs=[pl.BlockSpec((1, KV_WIDTH, n_buf), lambda b: (b, 0, 0)),
                  pl.BlockSpec((1, KV_WIDTH, NEW_PAD), lambda b: (b, 0, 0))],
        out_specs=pl.BlockSpec((1, KV_WIDTH, n_buf), lambda b: (b, 0, 0)),
        out_shape=jax.ShapeDtypeStruct((B, KV_WIDTH, n_buf), F32),
        compiler_params=_params("arbitrary"),
        name="win_shift",
    )(stT, new_tail)


ROUTER_PAD = 128


def _moe_kernel(h_ref, g_ref, wr_ref, br_ref, wg_ref, wu_ref, wd_ref, gf_ref, y_ref, xn_ref, gate_ref, acc_ref):
    e = pl.program_id(1)
    tm = h_ref.shape[0]
    lane = lax.broadcasted_iota(jnp.int32, (tm, ROUTER_PAD), 1)

    @pl.when(e == 0)
    def _():
        hn = _unit_rms(h_ref[...]) * g_ref[...]
        xn_ref[...] = hn.astype(xn_ref.dtype)
        logits = jnp.dot(hn, wr_ref[...], precision=HIGHEST, preferred_element_type=F32) + br_ref[...]
        logits = jnp.where(lane < N_EXPERTS, logits, NEG_INF)
        m1 = jnp.max(logits, axis=-1, keepdims=True)
        i1 = jnp.min(jnp.where(logits == m1, lane, ROUTER_PAD), axis=-1, keepdims=True)
        rest = jnp.where(lane == i1, NEG_INF, logits)
        m2 = jnp.max(rest, axis=-1, keepdims=True)
        i2 = jnp.min(jnp.where(rest == m2, lane, ROUTER_PAD), axis=-1, keepdims=True)
        e2 = jnp.exp(m2 - m1)
        denom = 1.0 + e2
        gate_ref[...] = jnp.where(lane == i1, 1.0 / denom, jnp.where(lane == i2, e2 / denom, -1.0))
        acc_ref[...] = jnp.zeros_like(acc_ref)

    xn = xn_ref[...]
    gg = _mm(xn, wg_ref[0])
    uu = _mm(xn, wu_ref[0])
    y = _mm(gg * jax.nn.sigmoid(gg) * uu, wd_ref[0])
    w_e = jnp.max(jnp.where(lane == e, gate_ref[...], -1.0), axis=-1, keepdims=True)
    acc_ref[...] += jnp.where(w_e >= 0.0, y * w_e, 0.0)

    @pl.when(e == pl.num_programs(1) - 1)
    def _():
        out = h_ref[...] + acc_ref[...]
        y_ref[...] = _unit_rms(out) * gf_ref[...]


def _moe_final(h, g, w_router, b_router, wg, wu, wd, g_final, tm):
    M = h.shape[0]
    return pl.pallas_call(
        _moe_kernel,
        grid=(M // tm, N_EXPERTS),
        in_specs=[pl.BlockSpec((tm, D_MODEL), lambda i, e: (i, 0)),
                  _full((1, D_MODEL)), _full((D_MODEL, ROUTER_PAD)), _full((1, ROUTER_PAD)),
                  pl.BlockSpec((1, D_MODEL, D_EXPERT), lambda i, e: (e, 0, 0)),
                  pl.BlockSpec((1, D_MODEL, D_EXPERT), lambda i, e: (e, 0, 0)),
                  pl.BlockSpec((1, D_EXPERT, D_MODEL), lambda i, e: (e, 0, 0)),
                  _full((1, D_MODEL))],
        out_specs=pl.BlockSpec((tm, D_MODEL), lambda i, e: (i, 0)),
        out_shape=jax.ShapeDtypeStruct((M, D_MODEL), F32),
        scratch_shapes=[pltpu.VMEM((tm, D_MODEL), wg.dtype), pltpu.VMEM((tm, ROUTER_PAD), F32),
                        pltpu.VMEM((tm, D_MODEL), F32)],
        compiler_params=_params("arbitrary", "arbitrary"),
        name="moe_final",
    )(h, g, w_router, b_router, wg, wu, wd, g_final)


def _spread_matrix(ncp, n_blk):
    a = np.zeros((ncp, n_blk), np.float32)
    per = SEL_BLOCK // CMP_STRIDE
    for n in range(ncp - 1):
        for c in (n, n + 1):
            if c // per < n_blk:
                a[n, c // per] += 0.5
    return a


def _gate_expand():
    e = np.zeros((N_BRANCH, GATE_PAD, NSA_WIDTH), np.float32)
    for hd in range(NSA_HEADS):
        for br in range(N_BRANCH):
            e[br, hd * N_BRANCH + br, hd * HEAD_DIM:(hd + 1) * HEAD_DIM] = 1.0
    return e


def _chunk_weights(w1):
    eye = jnp.eye(NSA_KV_HEADS, dtype=F32)
    lo = jnp.einsum("cdh,gk->cgdkh", w1[:CMP_STRIDE], eye).reshape(CMP_STRIDE, KV_WIDTH, KV_WIDTH)
    hi = jnp.einsum("cdh,gk->cgdkh", w1[CMP_STRIDE:], eye).reshape(CMP_STRIDE, KV_WIDTH, KV_WIDTH)
    return jnp.concatenate([lo, hi], axis=-1).astype(BF16)


def _to_rows(xT, heads):
    B, _, T = xT.shape
    return xT.reshape(B, heads, HEAD_DIM, T).transpose(0, 3, 1, 2)


def _to_cols(x):
    lead = x.shape[:-3]
    T, heads, hd = x.shape[-3:]
    n = len(lead)
    return x.transpose(*range(n), n + 1, n + 2, n).reshape(*lead, heads * hd, T)


def kernel(x_prompt, x_sample, cache_mem_k, cache_mem_v, state_conv, cache_cmp_k, cache_cmp_v, cache_sel_k, cache_sel_v, state_win_k, state_win_v, page_table, mem_prompt, g_mix, g_ffn, g_mem, w_mem_kv, w_in_a, conv_w, conv_b, ln_g, ln_b, w_in_b, g_kv, w_kv, cmp_pe, cmp_w1, cmp_w2, w_out, w_ff_gate, w_ff_up, w_ff_down, w_router, b_router, w_e_gate, w_e_up, w_e_down, g_final):
    BP, T, _ = x_prompt.shape
    BS, S, _ = x_sample.shape
    n_pages = page_table.shape[1]
    past_len = n_pages * PAGE_SIZE
    n_buf = state_win_k.shape[1]
    bf = lambda a: a.astype(BF16)

    w_memT = bf(jnp.swapaxes(w_mem_kv, 1, 2))
    n_gate = N_BRANCH * NSA_HEADS
    wb = w_in_b[0]
    w_b_pad = jnp.concatenate([wb[:, :NSA_WIDTH],
                               jnp.pad(wb[:, NSA_WIDTH:NSA_WIDTH + n_gate], ((0, 0), (0, GATE_PAD - n_gate))),
                               wb[:, NSA_WIDTH + n_gate:]], axis=1)

    def dense_weights(cast):
        return dict(a=cast(w_in_a[0]),
                    out_tok=[cast(w_out[l, :NSA_WIDTH]) for l in range(DEPTH)],
                    out_mem=[cast(w_out[l, NSA_WIDTH:]) for l in range(DEPTH)],
                    ff=(cast(w_ff_gate[0]), cast(w_ff_up[0]), cast(w_ff_down[0])),
                    b=cast(w_b_pad), kv=cast(w_kv))

    experts = (bf(w_e_gate[0]), bf(w_e_up[0]), bf(w_e_down[0]))
    WP = dense_weights(bf)
    WS = dense_weights(lambda a: a)
    w_kvT = bf(w_kv.T)
    w_cmp_rm = WP["kv"][:, :2 * KV_WIDTH]
    wck = _chunk_weights(cmp_w1[0])
    wcv = _chunk_weights(cmp_w1[1])
    pe_flat = cmp_pe.reshape(2, 1, CMP_BLOCK * HEAD_DIM)
    w1_flat = cmp_w1.reshape(2, CMP_BLOCK * HEAD_DIM, HEAD_DIM)
    eye = jnp.eye(NSA_KV_HEADS, dtype=F32)
    w2_bd = bf(jnp.einsum("whd,gk->wghkd", cmp_w2, eye).reshape(2, KV_WIDTH, KV_WIDTH))
    w_r = jnp.pad(w_router[0], ((0, 0), (0, ROUTER_PAD - N_EXPERTS)))
    b_r = jnp.pad(b_router, ((0, 0), (0, ROUTER_PAD - N_EXPERTS)))
    gm = [g_mix[l:l + 1] for l in range(DEPTH)]
    gf = [g_ffn[l:l + 1] for l in range(DEPTH)]
    g_kv2 = g_kv.reshape(1, D_MODEL)
    g_fin = g_final.reshape(1, D_MODEL)
    gate_expand = jnp.asarray(_gate_expand())

    def layer0(h, B, Tn, prev, memkT, memvT, W, tm, tt, tq, precise):
        ag, qm = _in_proj_a(h, gm[0], W["a"], tm)
        tok, tail = _conv_module(ag.reshape(B, Tn, 2 * C_CONV), prev, conv_w[0], conv_b, ln_g, ln_b, tt)
        mo = _mem_attend(qm.reshape(B, Tn, MEM_WIDTH), memkT, memvT, 0, tq, precise)
        h = _out_proj(h, [tok.reshape(B * Tn, C_CONV)], mo.reshape(B * Tn, MEM_WIDTH),
                      W["out_tok"][0], W["out_mem"][0], tm)
        h = _ffn(h, gf[0], *W["ff"], tm, 256 if precise else D_FF // 2)
        return h, tail

    MP = BP * T
    memkT_p, memvT_p = _memkv(mem_prompt, g_mem, w_memT)
    h = x_prompt.reshape(MP, D_MODEL)
    h, tail_p = layer0(h, BP, T, jnp.zeros((BP, CONV_TAIL, C_CONV), F32), memkT_p, memvT_p, WP, 512, 256, 256, False)

    ckT, cvT, skT, svT, wkT, wvT, c_rm, q, gates, qm = _proj_b_prompt(h, BP, T, g_kv2, gm[1], w_kvT, w_cmp_rm,
                                                                      WP["b"], 512)
    n_chunk = T // CMP_STRIDE
    lohi_k, lohi_v = _chunk_proj_prompt(c_rm[:, :KV_WIDTH].reshape(MP // CMP_STRIDE, CHUNK_IN),
                                        c_rm[:, KV_WIDTH:].reshape(MP // CMP_STRIDE, CHUNK_IN),
                                        wck.reshape(CHUNK_IN, CHUNK_OUT), wcv.reshape(CHUNK_IN, CHUNK_OUT), 512)
    kcT, vc = _cmp_finish(lohi_k.reshape(BP, n_chunk, CHUNK_OUT), lohi_v.reshape(BP, n_chunk, CHUNK_OUT),
                          pe_flat, w1_flat, w2_bd)
    q3 = q.reshape(BP, T, NSA_WIDTH)
    gates3 = gates.reshape(BP, T, GATE_PAD)
    spread_p = jnp.asarray(_spread_matrix(n_chunk, T // SEL_BLOCK))
    oc, sel = _cmp_attend_prompt(q3, gates3, kcT.reshape(BP, NSA_KV_HEADS, HEAD_DIM, n_chunk), vc, spread_p, 256)
    osel = _flash_prompt(q3, gates3, sel, skT, svT, 256, 256, "sel")
    owin = _flash_prompt(q3, gates3, None, wkT, wvT, 256, 256, "win")
    mo = _mem_attend(qm.reshape(BP, T, MEM_WIDTH), memkT_p, memvT_p, 1, 256)
    flat = lambda a: a.reshape(MP, -1)
    h = _out_proj(h, [flat(oc), flat(osel), flat(owin)], flat(mo), WP["out_tok"][1], WP["out_mem"][1], 512)
    y_prompt = _moe_final(h, gf[1], w_r, b_r, *experts, g_fin, 512).reshape(BP, T, D_MODEL)

    MS = BS * S
    memkT_s = _to_cols(cache_mem_k)
    memvT_s = _to_cols(cache_mem_v)
    hs = x_sample.reshape(MS, D_MODEL)
    hs, tail_s = layer0(hs, BS, S, state_conv[0], memkT_s, memvT_s, WS, MS, S, S, True)

    kv_s, qs, gates_s, qms = _proj_b_sample(hs, g_kv2, gm[1], WS["kv"], WS["b"])
    kv_new = [kv_s[:, i * KV_WIDTH:(i + 1) * KV_WIDTH].reshape(BS, S, NSA_KV_HEADS, HEAD_DIM)
              for i in range(2 * N_BRANCH)]

    pool = lambda c: c.transpose(0, 2, 3, 1)
    pool2 = lambda c: pool(c).reshape(c.shape[0], KV_WIDTH, PAGE_SIZE)
    lk, lv = _chunk_proj_sample(page_table, pool2(cache_cmp_k), pool2(cache_cmp_v), wck, wcv)
    kcT_s, vc_s = _cmp_finish(lk, lv, pe_flat, w1_flat, w2_bd)
    n_chunk_s = past_len // CMP_STRIDE
    n_blk_s = past_len // SEL_BLOCK
    n_pick = min(SEL_TOP_N - 1, n_blk_s)
    q5 = qs.reshape(BS, S, NSA_KV_HEADS, HEADS_PER_GROUP, HEAD_DIM).transpose(0, 2, 1, 3, 4)
    q8 = jnp.pad(q5, ((0, 0), (0, 0), (0, 0), (0, Q_SLOTS - HEADS_PER_GROUP), (0, 0)))
    q32 = q8.reshape(BS, NSA_KV_HEADS, Q_ROWS, HEAD_DIM)
    spread_s = jnp.asarray(_spread_matrix(n_chunk_s, n_blk_s))
    oc_s, idx = _cmp_attend_sample(q32, kcT_s.reshape(BS, NSA_KV_HEADS, HEAD_DIM, n_chunk_s), vc_s, spread_s, n_pick)

    def new_tile(x):
        return jnp.pad(x.transpose(0, 2, 3, 1), ((0, 0), (0, 0), (0, 0), (0, NEW_PAD - S)))

    idx_flat = idx[..., :SEL_TOP_N].reshape(-1)
    os_s = _sel_attend_sample(idx_flat, page_table.reshape(-1), q8, pool(cache_sel_k), pool(cache_sel_v),
                              new_tile(kv_new[2]), new_tile(kv_new[3]), n_pages, n_pick)
    bufkT = _to_cols(state_win_k)
    bufvT = _to_cols(state_win_v)
    ow_s = _win_attend_sample(q32, bufkT.reshape(BS, NSA_KV_HEADS, HEAD_DIM, n_buf),
                              bufvT.reshape(BS, NSA_KV_HEADS, HEAD_DIM, n_buf),
                              new_tile(kv_new[4]), new_tile(kv_new[5]))

    def heads_to_rows(o32):
        o = o32.reshape(BS, NSA_KV_HEADS, S, Q_SLOTS, HEAD_DIM)[:, :, :, :HEADS_PER_GROUP]
        return o.transpose(0, 2, 1, 3, 4).reshape(MS, NSA_WIDTH)

    os_rows = os_s[:, :, :, :HEADS_PER_GROUP].transpose(0, 2, 1, 3, 4).reshape(MS, NSA_WIDTH)
    tok_s = _gate_mix(gates_s, heads_to_rows(oc_s), os_rows, heads_to_rows(ow_s), gate_expand)
    mo_s = _mem_attend(qms.reshape(BS, S, MEM_WIDTH), memkT_s, memvT_s, 1, S, True)
    hs = _out_proj(hs, [tok_s], mo_s.reshape(MS, MEM_WIDTH), WS["out_tok"][1], WS["out_mem"][1], MS)
    y_sample = _moe_final(hs, gf[1], w_r, b_r, *experts, g_fin, MS).reshape(BS, S, D_MODEL)

    def new_tail(x):
        return jnp.pad(_to_cols(x), ((0, 0), (0, 0), (NEW_PAD - S, 0)))

    keep_p = min(WINDOW, T)
    if n_buf + S > WINDOW:
        win_k_s = _to_rows(_win_shift(bufkT, new_tail(kv_new[4])), NSA_KV_HEADS)
        win_v_s = _to_rows(_win_shift(bufvT, new_tail(kv_new[5])), NSA_KV_HEADS)
    else:
        win_k_s = jnp.concatenate([state_win_k, kv_new[4]], axis=1)
        win_v_s = jnp.concatenate([state_win_v, kv_new[5]], axis=1)

    rows = lambda xT: _to_rows(xT, NSA_KV_HEADS)
    mem_k_p = _to_rows(memkT_p.reshape(DEPTH * BP, MEM_WIDTH, MEM_LEN), 4).reshape(DEPTH, BP, MEM_LEN, 4, HEAD_DIM)
    mem_v_p = _to_rows(memvT_p.reshape(DEPTH * BP, MEM_WIDTH, MEM_LEN), 4).reshape(DEPTH, BP, MEM_LEN, 4, HEAD_DIM)
    return (y_prompt, y_sample, mem_k_p, mem_v_p, tail_p[None], tail_s[None],
            rows(ckT), rows(cvT), rows(skT), rows(svT),
            kv_new[0], kv_new[1], kv_new[2], kv_new[3],
            rows(wkT[:, :, T - keep_p:]), rows(wvT[:, :, T - keep_p:]), win_k_s, win_v_s)
```
